```python
import jax, jax.numpy as jnp
from jax import lax
import numpy as np

D_MODEL = 1024
BATCH = 8
SEQ = 2048
DEPTH = 4
DEC_BATCH = 128
DEC_SEQ = 4
PAST_LEN = 16384
PAGE_SIZE = 128

N_META = 16
POOL_WINDOWS = (2, 4, 8, 16)
N_POOL_GROUPS = len(POOL_WINDOWS)
POOL_GROUP = D_MODEL // N_POOL_GROUPS
POOL_CTX = max(POOL_WINDOWS) - 1
CONV_WIDTH = 31
CONV_CTX = CONV_WIDTH - 1
D_FF = ((8 * D_MODEL // 3 + 255) // 256) * 256
N_EXPERTS = 8
TOP_K = 2
N_POOL_LAYERS = (DEPTH + 1) // 2
N_CONV_LAYERS = DEPTH // 2
DEEPNORM_ALPHA = float((2 * DEPTH) ** 0.25)
DEEPNORM_BETA = float((8 * DEPTH) ** -0.25)
LN_EPS = 1e-5

kernel_name = "hybrid_pool_conformer_deepnorm_moe_step"


def layer_norm(x, g, b):
    xf = x.astype(jnp.float32)
    mu = jnp.mean(xf, axis=-1, keepdims=True)
    xc = xf - mu
    var = jnp.mean(xc * xc, axis=-1, keepdims=True)
    y = xc * lax.rsqrt(var + LN_EPS) * g.astype(jnp.float32) + b.astype(jnp.float32)
    return y.astype(x.dtype)


def pool_mixer(x, prev, pos0, w, scale):
    T = x.shape[1]
    ext = jnp.concatenate([prev.astype(x.dtype), x], axis=1)
    cs = jnp.cumsum(ext.astype(jnp.float32), axis=1)
    cs0 = jnp.pad(cs, ((0, 0), (1, 0), (0, 0)))
    pos = pos0 + jnp.arange(T)
    xf = x.astype(jnp.float32)
    diffs = []
    for g, wg in enumerate(POOL_WINDOWS):
        lo, hi = g * POOL_GROUP, (g + 1) * POOL_GROUP
        s = cs0[:, POOL_CTX + 1:POOL_CTX + 1 + T, lo:hi] - cs0[:, POOL_CTX + 1 - wg:POOL_CTX + 1 - wg + T, lo:hi]
        cnt = jnp.minimum(pos + 1, wg).astype(jnp.float32)[None, :, None]
        diffs.append(s / cnt - xf[..., lo:hi])
    d = jnp.stack(diffs, axis=2)
    y = jnp.einsum('btgc,gce->btge', d, w.astype(jnp.float32)).reshape(x.shape)
    y = y * scale.astype(jnp.float32)
    return y.astype(x.dtype), ext[:, -POOL_CTX:]


def conv_module(x, prev, w1, b1, dw, dwb, lng, lnb, w2, b2):
    h = jnp.dot(x, w1) + b1
    a, gt = jnp.split(h, 2, axis=-1)
    u = a * jax.nn.sigmoid(gt)
    ext = jnp.concatenate([prev.astype(u.dtype), u], axis=1)
    c = lax.conv_general_dilated(ext, dw.astype(ext.dtype)[:, None, :], window_strides=(1,), padding='VALID',
                                 dimension_numbers=('NWC', 'WIO', 'NWC'), feature_group_count=D_MODEL)
    c = layer_norm(c + dwb, lng, lnb)
    y = jnp.dot(jax.nn.silu(c), w2) + b2
    return y.astype(x.dtype), ext[:, -CONV_CTX:]


def swiglu(x, wg, wu, wd):
    return jnp.dot(jax.nn.silu(jnp.dot(x, wg)) * jnp.dot(x, wu), wd)


def moe_swiglu(x, router, wg, wu, wd):
    shp = x.shape
    xf = x.reshape(-1, shp[-1])
    logits = jnp.dot(xf, router).astype(jnp.float32)
    top_v, top_i = lax.top_k(logits, TOP_K)
    probs = jax.nn.softmax(top_v, axis=-1)
    gate = jnp.sum(jax.nn.one_hot(top_i, N_EXPERTS, dtype=jnp.float32) * probs[..., None], axis=1)
    y = jnp.zeros(xf.shape, jnp.float32)
    for e in range(N_EXPERTS):
        y = y + gate[:, e:e + 1] * swiglu(xf, wg[e], wu[e], wd[e]).astype(jnp.float32)
    return y.astype(x.dtype).reshape(shp)


def trunk(x, pool_prev, conv_prev, pos0, pool_w, pool_scale, conv_w1, conv_b1, conv_dw, conv_dw_b,
          conv_ln_g, conv_ln_b, conv_w2, conv_b2, ln_mix_g, ln_mix_b, ln_ffn_g, ln_ffn_b,
          ffn_w_gate, ffn_w_up, ffn_w_down, moe_router, moe_w_gate, moe_w_up, moe_w_down):
    new_pool, new_conv = [], []
    for i in range(DEPTH):
        j = i // 2
        if i % 2 == 0:
            h, s = pool_mixer(x, pool_prev[j], pos0, pool_w[j], pool_scale[j])
            new_pool.append(s)
        else:
            h, s = conv_module(x, conv_prev[j], conv_w1[j], conv_b1[j], conv_dw[j], conv_dw_b[j],
                               conv_ln_g[j], conv_ln_b[j], conv_w2[j], conv_b2[j])
            new_conv.append(s)
        x = layer_norm(DEEPNORM_ALPHA * x + h, ln_mix_g[i], ln_mix_b[i])
        if i % 2 == 0:
            f = swiglu(x, ffn_w_gate[j], ffn_w_up[j], ffn_w_down[j]).astype(x.dtype)
        else:
            f = moe_swiglu(x, moe_router[j], moe_w_gate[j], moe_w_up[j], moe_w_down[j])
        x = layer_norm(DEEPNORM_ALPHA * x + f, ln_ffn_g[i], ln_ffn_b[i])
    return x, jnp.stack(new_pool), jnp.stack(new_conv)


def setup_inputs(seed: int = 0) -> dict:
    key = jax.random.key(seed)
    ks = jax.random.split(key, 32)
    n = lambda k, shape, s: jax.random.normal(k, shape, jnp.float32) * s
    D, F, E = D_MODEL, D_FF, N_EXPERTS
    NP, NC = N_POOL_LAYERS, N_CONV_LAYERS
    return {
        "x_prompt": n(ks[0], (BATCH, SEQ, D), 1.0),
        "x_sample": n(ks[1], (DEC_BATCH, DEC_SEQ, D), 1.0),
        "state_pool": n(ks[2], (NP, DEC_BATCH, POOL_CTX, D), 1.0),
        "state_conv": n(ks[3], (NC, DEC_BATCH, CONV_CTX, D), 0.5),
        "meta": n(ks[4], (N_META, D), 1.0),
        "pool_w": n(ks[5], (NP, N_POOL_GROUPS, POOL_GROUP, POOL_GROUP), DEEPNORM_BETA * POOL_GROUP ** -0.5),
        "pool_scale": 1.0 + n(ks[6], (NP, D), 0.1),
        "conv_w1": n(ks[7], (NC, D, 2 * D), D ** -0.5),
        "conv_b1": n(ks[8], (NC, 2 * D), 0.02),
        "conv_dw": n(ks[9], (NC, CONV_WIDTH, D), CONV_WIDTH ** -0.5),
        "conv_dw_b": n(ks[10], (NC, D), 0.02),
        "conv_ln_g": 1.0 + n(ks[11], (NC, D), 0.05),
        "conv_ln_b": n(ks[12], (NC, D), 0.02),
        "conv_w2": n(ks[13], (NC, D, D), DEEPNORM_BETA * D ** -0.5),
        "conv_b2": n(ks[14], (NC, D), 0.02),
        "ln_mix_g": 1.0 + n(ks[15], (DEPTH, D), 0.05),
        "ln_mix_b": n(ks[16], (DEPTH, D), 0.02),
        "ln_ffn_g": 1.0 + n(ks[17], (DEPTH, D), 0.05),
        "ln_ffn_b": n(ks[18], (DEPTH, D), 0.02),
        "ffn_w_gate": n(ks[19], (NP, D, F), D ** -0.5),
        "ffn_w_up": n(ks[20], (NP, D, F), D ** -0.5),
        "ffn_w_down": n(ks[21], (NP, F, D), DEEPNORM_BETA * F ** -0.5),
        "moe_router": n(ks[22], (NC, D, E), D ** -0.5),
        "moe_w_gate": n(ks[23], (NC, E, D, F), D ** -0.5),
        "moe_w_up": n(ks[24], (NC, E, D, F), D ** -0.5),
        "moe_w_down": n(ks[25], (NC, E, F, D), DEEPNORM_BETA * F ** -0.5),
    }


def reference(x_prompt, x_sample, state_pool, state_conv, meta, pool_w, pool_scale, conv_w1, conv_b1,
              conv_dw, conv_dw_b, conv_ln_g, conv_ln_b, conv_w2, conv_b2, ln_mix_g, ln_mix_b,
              ln_ffn_g, ln_ffn_b, ffn_w_gate, ffn_w_up, ffn_w_down, moe_router, moe_w_gate,
              moe_w_up, moe_w_down):
    weights = (pool_w, pool_scale, conv_w1, conv_b1, conv_dw, conv_dw_b, conv_ln_g, conv_ln_b,
               conv_w2, conv_b2, ln_mix_g, ln_mix_b, ln_ffn_g, ln_ffn_b, ffn_w_gate, ffn_w_up,
               ffn_w_down, moe_router, moe_w_gate, moe_w_up, moe_w_down)
    B = x_prompt.shape[0]
    meta_b = jnp.broadcast_to(meta.astype(x_prompt.dtype)[None], (B, N_META, D_MODEL))
    xp = jnp.concatenate([meta_b, x_prompt], axis=1)
    zero_pool = jnp.zeros((N_POOL_LAYERS, B, POOL_CTX, D_MODEL), x_prompt.dtype)
    zero_conv = jnp.zeros((N_CONV_LAYERS, B, CONV_CTX, D_MODEL), x_prompt.dtype)
    yp, pool_p, conv_p = trunk(xp, zero_pool, zero_conv, 0, *weights)
    ys, pool_s, conv_s = trunk(x_sample, state_pool, state_conv, PAST_LEN, *weights)
    y_prompt = yp[:, N_META:]
    return (y_prompt, ys, pool_p, conv_p, pool_s, conv_s)
```

```python
import functools

import jax
import jax.numpy as jnp
from jax import lax
from jax.experimental import pallas as pl
from jax.experimental.pallas import tpu as pltpu

F32 = jnp.float32
BF16 = jnp.bfloat16

POOL_WINDOWS = (2, 4, 8, 16)
PAST_LEN = 16384
LN_EPS = 1e-5
LANES = 128
VMEM_LIMIT = 56 * 1024 * 1024
NEG = -1e30


def _cparams(sem):
    return pltpu.CompilerParams(dimension_semantics=sem, vmem_limit_bytes=VMEM_LIMIT)


def _pick(n, cands):
    for c in cands:
        if n % c == 0:
            return c
    raise ValueError(f"no tile for {n} in {cands}")


def _ln(z, g, b):
    mu = jnp.mean(z, axis=-1, keepdims=True)
    zc = z - mu
    var = jnp.mean(zc * zc, axis=-1, keepdims=True)
    return zc * lax.rsqrt(var + LN_EPS) * g + b


def _dot(a, b):
    return jnp.dot(a, b, preferred_element_type=F32)


def _pool_prompt_kernel(x_ref, w_ref, sc_ref, g_ref, b_ref, o_ref, ext_ref, *, tt, ctx, alpha):
    t = pl.program_id(1)
    d_model = x_ref.shape[-1]
    gc = d_model // len(POOL_WINDOWS)

    @pl.when(t == 0)
    def _():
        ext_ref[0:ctx, :] = jnp.zeros((ctx, d_model), F32)

    @pl.when(t > 0)
    def _():
        ext_ref[0:ctx, :] = ext_ref[tt:tt + ctx, :]

    x = x_ref[...]
    ext_ref[ctx:ctx + tt, :] = x
    pos = t * tt + lax.broadcasted_iota(jnp.int32, (tt, 1), 0)
    ys = []
    for gi, w in enumerate(POOL_WINDOWS):
        lo, hi = gi * gc, (gi + 1) * gc
        s = ext_ref[ctx:ctx + tt, lo:hi]
        for j in range(1, w):
            s = s + ext_ref[ctx - j:ctx - j + tt, lo:hi]
        cnt = jnp.minimum(pos + 1, w).astype(F32)
        dlt = s / cnt - x[:, lo:hi]
        ys.append(_dot(dlt.astype(BF16), w_ref[gi].astype(BF16)))
    y = jnp.concatenate(ys, axis=1) * sc_ref[...]
    o_ref[...] = _ln(alpha * x + y, g_ref[...], b_ref[...])


def _pool_prompt(x, n_out, bsz, tp, w, sc, g, b, alpha):
    d = x.shape[-1]
    tt = _pick(tp, (688, 512, 256, 128, 64, 32, 16))
    nt = tp // tt
    ctx = 16
    kern = functools.partial(_pool_prompt_kernel, tt=tt, ctx=ctx, alpha=alpha)
    vec = pl.BlockSpec((1, d), lambda i, t: (0, 0))
    return pl.pallas_call(
        kern,
        grid=(bsz, nt),
        in_specs=[pl.BlockSpec((tt, d), lambda i, t: (i * nt + t, 0)),
                  pl.BlockSpec(w.shape, lambda i, t: (0, 0, 0)),
                  vec, vec, vec],
        out_specs=pl.BlockSpec((tt, d), lambda i, t: (i * nt + t, 0)),
        out_shape=jax.ShapeDtypeStruct((n_out, d), F32),
        scratch_shapes=[pltpu.VMEM((tt + ctx, d), F32)],
        compiler_params=_cparams(("arbitrary", "arbitrary")),
        name="pool_prompt",
    )(x, w, sc, g, b)


def _pool_sample_kernel(st_ref, x_ref, w_ref, sc_ref, g_ref, b_ref, o_ref, *, alpha):
    ctx, sb, d_model = st_ref.shape
    ts = x_ref.shape[0]
    gc = d_model // len(POOL_WINDOWS)

    def row(i, lo, hi):
        return st_ref[i, :, lo:hi] if i < ctx else x_ref[i - ctx, :, lo:hi]

    ys = []
    for gi, w in enumerate(POOL_WINDOWS):
        lo, hi = gi * gc, (gi + 1) * gc
        ds = []
        for t in range(ts):
            s = row(ctx + t, lo, hi)
            for j in range(1, w):
                s = s + row(ctx + t - j, lo, hi)
            cnt = float(min(PAST_LEN + t + 1, w))
            ds.append(s / cnt - x_ref[t, :, lo:hi])
        dlt = jnp.concatenate(ds, axis=0)
        ys.append(_dot(dlt.astype(BF16), w_ref[gi].astype(BF16)))
    y = jnp.concatenate(ys, axis=1) * sc_ref[...]
    x = jnp.concatenate([x_ref[t] for t in range(ts)], axis=0)
    out = _ln(alpha * x + y, g_ref[...], b_ref[...])
    for t in range(ts):
        o_ref[t] = out[t * sb:(t + 1) * sb]


def _pool_sample(st, x, w, sc, g, b, alpha):
    ctx, ns, d = st.shape
    ts = x.shape[0]
    sb = _pick(ns, (32, 16, 8))
    kern = functools.partial(_pool_sample_kernel, alpha=alpha)
    vec = pl.BlockSpec((1, d), lambda i: (0, 0))
    return pl.pallas_call(
        kern,
        grid=(ns // sb,),
        in_specs=[pl.BlockSpec((ctx, sb, d), lambda i: (0, i, 0)),
                  pl.BlockSpec((ts, sb, d), lambda i: (0, i, 0)),
                  pl.BlockSpec(w.shape, lambda i: (0, 0, 0)),
                  vec, vec, vec],
        out_specs=pl.BlockSpec((ts, sb, d), lambda i: (0, i, 0)),
        out_shape=jax.ShapeDtypeStruct((ts, ns, d), F32),
        compiler_params=_cparams(("arbitrary",)),
        name="pool_sample",
    )(st, x, w, sc, g, b)


def _swiglu_part(xb, wg_ref, wu_ref, wd_ref):
    hg = _dot(xb, wg_ref[...].astype(BF16))
    hu = _dot(xb, wu_ref[...].astype(BF16))
    a = (hg * jax.nn.sigmoid(hg) * hu).astype(BF16)
    return _dot(a, wd_ref[...].astype(BF16))


def _ffn_ln_kernel(x_ref, wg_ref, wu_ref, wd_ref, g_ref, b_ref, o_ref, xb_ref, acc_ref, *, alpha):
    f = pl.program_id(1)
    nf = pl.num_programs(1)

    @pl.when(f == 0)
    def _():
        xb_ref[...] = x_ref[...].astype(BF16)

    part = _swiglu_part(xb_ref[...], wg_ref, wu_ref, wd_ref)

    @pl.when(f == 0)
    def _():
        acc_ref[...] = part

    @pl.when(f > 0)
    def _():
        acc_ref[...] += part

    @pl.when(f == nf - 1)
    def _():
        o_ref[...] = _ln(alpha * x_ref[...] + acc_ref[...], g_ref[...], b_ref[...])


def _ffn_dense(x, wg, wu, wd, g, b, alpha):
    n, d = x.shape
    ff = wg.shape[1]
    tm = _pick(n, (896, 512, 256, 128, 64, 32, 16, 8))
    fc = _pick(ff, (256, 128))
    kern = functools.partial(_ffn_ln_kernel, alpha=alpha)
    vec = pl.BlockSpec((1, d), lambda i, f: (0, 0))
    return pl.pallas_call(
        kern,
        grid=(n // tm, ff // fc),
        in_specs=[pl.BlockSpec((tm, d), lambda i, f: (i, 0)),
                  pl.BlockSpec((d, fc), lambda i, f: (0, f)),
                  pl.BlockSpec((d, fc), lambda i, f: (0, f)),
                  pl.BlockSpec((fc, d), lambda i, f: (f, 0)),
                  vec, vec],
        out_specs=pl.BlockSpec((tm, d), lambda i, f: (i, 0)),
        out_shape=jax.ShapeDtypeStruct((n, d), F32),
        scratch_shapes=[pltpu.VMEM((tm, d), BF16), pltpu.VMEM((tm, d), F32)],
        compiler_params=_cparams(("arbitrary", "arbitrary")),
        name="ffn_dense",
    )(x, wg, wu, wd, g, b)


def _moe_ffn_kernel(te_ref, rb_ref, act_ref, x_ref, wg_ref, wu_ref, wd_ref, o_ref, xb_ref):
    t = pl.program_id(0)
    f = pl.program_id(1)

    @pl.when(act_ref[t] == 1)
    def _():
        @pl.when(f == 0)
        def _():
            xb_ref[...] = x_ref[...].astype(BF16)

        part = _swiglu_part(xb_ref[...], wg_ref, wu_ref, wd_ref)

        @pl.when(f == 0)
        def _():
            o_ref[...] = part

        @pl.when(f > 0)
        def _():
            o_ref[...] += part


def _moe_ffn(te, rb, act, xs, wg, wu, wd, tms):
    p, d = xs.shape
    ff = wg.shape[2]
    fc = _pick(ff, (256, 128))
    nf = ff // fc

    def fidx(t, f, act):
        return jnp.where(act[t] == 1, f, nf - 1)

    grid_spec = pltpu.PrefetchScalarGridSpec(
        num_scalar_prefetch=3,
        grid=(p // tms, nf),
        in_specs=[pl.BlockSpec((tms, d), lambda t, f, te, rb, act: (rb[t], 0)),
                  pl.BlockSpec((None, d, fc), lambda t, f, te, rb, act: (te[t], 0, fidx(t, f, act))),
                  pl.BlockSpec((None, d, fc), lambda t, f, te, rb, act: (te[t], 0, fidx(t, f, act))),
                  pl.BlockSpec((None, fc, d), lambda t, f, te, rb, act: (te[t], fidx(t, f, act), 0))],
        out_specs=pl.BlockSpec((tms, d), lambda t, f, te, rb, act: (rb[t], 0)),
        scratch_shapes=[pltpu.VMEM((tms, d), BF16)],
    )
    return pl.pallas_call(
        _moe_ffn_kernel,
        grid_spec=grid_spec,
        out_shape=jax.ShapeDtypeStruct((p, d), F32),
        compiler_params=_cparams(("arbitrary", "arbitrary")),
        name="moe_ffn",
    )(te, rb, act, xs, wg, wu, wd)


def _glu_kernel(x_ref, w_ref, b_ref, o_ref, wb_ref):
    d_model = x_ref.shape[-1]

    @pl.when(pl.program_id(0) == 0)
    def _():
        wb_ref[...] = w_ref[...].astype(BF16)

    xb = x_ref[...].astype(BF16)
    cw = min(256, d_model)
    for j in range(d_model // cw):
        lo, hi = j * cw, (j + 1) * cw
        a = _dot(xb, wb_ref[:, lo:hi]) + b_ref[:, lo:hi]
        gt = _dot(xb, wb_ref[:, d_model + lo:d_model + hi]) + b_ref[:, d_model + lo:d_model + hi]
        o_ref[:, lo:hi] = a * jax.nn.sigmoid(gt)


def _glu(x, w1, b1):
    n, d = x.shape
    tm = _pick(n, (896, 512, 256, 128, 64, 32, 16, 8))
    return pl.pallas_call(
        _glu_kernel,
        grid=(n // tm,),
        in_specs=[pl.BlockSpec((tm, d), lambda i: (i, 0)),
                  pl.BlockSpec((d, 2 * d), lambda i: (0, 0)),
                  pl.BlockSpec((1, 2 * d), lambda i: (0, 0))],
        out_specs=pl.BlockSpec((tm, d), lambda i: (i, 0)),
        out_shape=jax.ShapeDtypeStruct((n, d), F32),
        scratch_shapes=[pltpu.VMEM((d, 2 * d), BF16)],
        compiler_params=_cparams(("arbitrary",)),
        name="conv_glu",
    )(x, w1, b1)


def _conv_tail(c, x, dwb, lng, lnb, w2b, b2, g, b, alpha):
    c = _ln(c + dwb, lng, lnb)
    s = (c * jax.nn.sigmoid(c)).astype(BF16)
    y = _dot(s, w2b) + b2
    return _ln(alpha * x + y, g, b)


def _conv_prompt_kernel(u_ref, x_ref, dw_ref, dwb_ref, lng_ref, lnb_ref, w2_ref, b2_ref, g_ref, b_ref,
                        o_ref, ext_ref, c_ref, w2b_ref, *, tt, ctx, rc, alpha):
    i = pl.program_id(0)
    t = pl.program_id(1)
    d_model = x_ref.shape[-1]
    kw = dw_ref.shape[0]
    off = ctx - (kw - 1)

    @pl.when((i == 0) & (t == 0))
    def _():
        w2b_ref[...] = w2_ref[...].astype(BF16)

    @pl.when(t == 0)
    def _():
        ext_ref[0:ctx, :] = jnp.zeros((ctx, d_model), F32)

    @pl.when(t > 0)
    def _():
        ext_ref[0:ctx, :] = ext_ref[tt:tt + ctx, :]

    ext_ref[ctx:ctx + tt, :] = u_ref[...]
    for r0 in range(0, tt, rc):
        acc = ext_ref[r0 + off:r0 + off + rc, :] * dw_ref[0:1, :]
        for k in range(1, kw):
            acc = acc + ext_ref[r0 + off + k:r0 + off + k + rc, :] * dw_ref[k:k + 1, :]
        c_ref[r0:r0 + rc, :] = acc
    o_ref[...] = _conv_tail(c_ref[...], x_ref[...], dwb_ref[...], lng_ref[...], lnb_ref[...],
                            w2b_ref[...], b2_ref[...], g_ref[...], b_ref[...], alpha)


def _conv_prompt(u, x, n_out, bsz, tp, dw, dwb, lng, lnb, w2, b2, g, b, alpha):
    d = x.shape[-1]
    tt = _pick(tp, (688, 512, 256, 128, 64, 32))
    nt = tp // tt
    ctx = 32
    rc = 16
    kern = functools.partial(_conv_prompt_kernel, tt=tt, ctx=ctx, rc=rc, alpha=alpha)
    vec = pl.BlockSpec((1, d), lambda i, t: (0, 0))
    tile = pl.BlockSpec((tt, d), lambda i, t: (i * nt + t, 0))
    return pl.pallas_call(
        kern,
        grid=(bsz, nt),
        in_specs=[tile, tile,
                  pl.BlockSpec(dw.shape, lambda i, t: (0, 0)),
                  vec, vec, vec,
                  pl.BlockSpec((d, d), lambda i, t: (0, 0)),
                  vec, vec, vec],
        out_specs=tile,
        out_shape=jax.ShapeDtypeStruct((n_out, d), F32),
        scratch_shapes=[pltpu.VMEM((tt + ctx, d), F32), pltpu.VMEM((tt, d), F32),
                        pltpu.VMEM((d, d), BF16)],
        compiler_params=_cparams(("arbitrary", "arbitrary")),
        name="conv_prompt",
    )(u, x, dw, dwb, lng, lnb, w2, b2, g, b)


def _conv_sample_kernel(st_ref, u_ref, x_ref, dw_ref, dwb_ref, lng_ref, lnb_ref, w2_ref, b2_ref,
                        g_ref, b_ref, o_ref, *, alpha):
    ctx, sb, d_model = st_ref.shape
    ts = u_ref.shape[0]
    kw = dw_ref.shape[0]

    def row(i):
        return st_ref[i] if i < ctx else u_ref[i - ctx]

    cs = []
    for t in range(ts):
        acc = row(t) * dw_ref[0:1, :]
        for k in range(1, kw):
            acc = acc + row(t + k) * dw_ref[k:k + 1, :]
        cs.append(acc)
    c = jnp.concatenate(cs, axis=0)
    x = jnp.concatenate([x_ref[t] for t in range(ts)], axis=0)
    out = _conv_tail(c, x, dwb_ref[...], lng_ref[...], lnb_ref[...], w2_ref[...].astype(BF16),
                     b2_ref[...], g_ref[...], b_ref[...], alpha)
    for t in range(ts):
        o_ref[t] = out[t * sb:(t + 1) * sb]


def _conv_sample(st, u, x, dw, dwb, lng, lnb, w2, b2, g, b, alpha):
    ctx, ns, d = st.shape
    ts = u.shape[0]
    sb = _pick(ns, (32, 16, 8))
    kern = functools.partial(_conv_sample_kernel, alpha=alpha)
    vec = pl.BlockSpec((1, d), lambda i: (0, 0))
    blk = pl.BlockSpec((ts, sb, d), lambda i: (0, i, 0))
    return pl.pallas_call(
        kern,
        grid=(ns // sb,),
        in_specs=[pl.BlockSpec((ctx, sb, d), lambda i: (0, i, 0)), blk, blk,
                  pl.BlockSpec(dw.shape, lambda i: (0, 0)),
                  vec, vec, vec,
                  pl.BlockSpec((d, d), lambda i: (0, 0)),
                  vec, vec, vec],
        out_specs=blk,
        out_shape=jax.ShapeDtypeStruct((ts, ns, d), F32),
        compiler_params=_cparams(("arbitrary",)),
        name="conv_sample",
    )(st, u, x, dw, dwb, lng, lnb, w2, b2, g, b)


def _route_kernel(x_ref, r_ref, route_ref, cnt_ref, tri_ref, carry_ref, *, n_exp):
    i = pl.program_id(0)
    tm = x_ref.shape[0]

    @pl.when(i == 0)
    def _():
        carry_ref[...] = jnp.zeros(carry_ref.shape, F32)
        rr = lax.broadcasted_iota(jnp.int32, (tm, tm), 0)
        cc = lax.broadcasted_iota(jnp.int32, (tm, tm), 1)
        tri_ref[...] = jnp.where(rr > cc, 1.0, 0.0).astype(BF16)

    x = x_ref[...]
    r = r_ref[...]
    xh = x.astype(BF16)
    xl = (x - xh.astype(F32)).astype(BF16)
    rh = r.astype(BF16)
    rl = (r - rh.astype(F32)).astype(BF16)
    logits = _dot(xh, rh) + (_dot(xl, rh) + _dot(xh, rl))
    lane = lax.broadcasted_iota(jnp.int32, (tm, LANES), 1).astype(F32)
    l1 = jnp.where(lane < n_exp, logits, NEG)
    m1 = jnp.max(l1, axis=-1, keepdims=True)
    i1 = jnp.min(jnp.where(l1 == m1, lane, float(LANES)), axis=-1, keepdims=True)
    l2 = jnp.where(lane == i1, NEG, l1)
    m2 = jnp.max(l2, axis=-1, keepdims=True)
    i2 = jnp.min(jnp.where(l2 == m2, lane, float(LANES)), axis=-1, keepdims=True)
    ex = jnp.exp(m2 - m1)
    p1 = 1.0 / (1.0 + ex)
    p2 = ex * p1
    oh1 = lane == i1
    oh2 = lane == i2
    m = jnp.where(oh1 | oh2, 1.0, 0.0)
    before = _dot(tri_ref[...], m.astype(BF16)) + carry_ref[...]
    rank1 = jnp.sum(jnp.where(oh1, before, 0.0), axis=-1, keepdims=True)
    rank2 = jnp.sum(jnp.where(oh2, before, 0.0), axis=-1, keepdims=True)
    carry_ref[...] += jnp.sum(m, axis=0, keepdims=True)
    route = jnp.where(lane == 0, i1, 0.0)
    for k, v in enumerate((i2, p1, p2, rank1, rank2), start=1):
        route = jnp.where(lane == k, v, route)
    route_ref[...] = route
    cnt_ref[...] = carry_ref[...]


def _route(x, router):
    n, d = x.shape
    n_exp = router.shape[1]
    tm = _pick(n, (896, 512, 256, 128, 64, 32, 16, 8))
    rp = jnp.pad(router, ((0, 0), (0, LANES - n_exp)))
    kern = functools.partial(_route_kernel, n_exp=n_exp)
    return pl.pallas_call(
        kern,
        grid=(n // tm,),
        in_specs=[pl.BlockSpec((tm, d), lambda i: (i, 0)),
                  pl.BlockSpec((d, LANES), lambda i: (0, 0))],
        out_specs=[pl.BlockSpec((tm, LANES), lambda i: (i, 0)),
                   pl.BlockSpec((1, LANES), lambda i: (0, 0))],
        out_shape=[jax.ShapeDtypeStruct((n, LANES), F32), jax.ShapeDtypeStruct((1, LANES), F32)],
        scratch_shapes=[pltpu.VMEM((tm, tm), BF16), pltpu.VMEM((1, LANES), F32)],
        compiler_params=_cparams(("arbitrary",)),
        name="route",
    )(x, rp)


def _combine_ln_kernel(x_ref, g1_ref, g2_ref, route_ref, g_ref, b_ref, o_ref, *, alpha):
    p1 = route_ref[:, 2:3]
    p2 = route_ref[:, 3:4]
    y = p1 * g1_ref[...] + p2 * g2_ref[...]
    o_ref[...] = _ln(alpha * x_ref[...] + y, g_ref[...], b_ref[...])


def _combine_ln(x, g1, g2, route, g, b, alpha):
    n, d = x.shape
    tm = _pick(n, (896, 512, 256, 128, 64, 32, 16, 8))
    kern = functools.partial(_combine_ln_kernel, alpha=alpha)
    tile = pl.BlockSpec((tm, d), lambda i: (i, 0))
    vec = pl.BlockSpec((1, d), lambda i: (0, 0))
    return pl.pallas_call(
        kern,
        grid=(n // tm,),
        in_specs=[tile, tile, tile, pl.BlockSpec((tm, LANES), lambda i: (i, 0)), vec, vec],
        out_specs=tile,
        out_shape=jax.ShapeDtypeStruct((n, d), F32),
        compiler_params=_cparams(("arbitrary",)),
        name="combine_ln",
    )(x, g1, g2, route, g, b)


def _moe(x, router, wg, wu, wd, g, b, alpha):
    n, d = x.shape
    n_exp = router.shape[1]
    tms = 1024 if n >= 4096 else 32
    route, cnt = _route(x, router)
    e1 = route[:, 0].astype(jnp.int32)
    e2 = route[:, 1].astype(jnp.int32)
    counts = cnt[0, :n_exp].astype(jnp.int32)
    padded = ((counts + tms - 1) // tms) * tms
    ends = jnp.cumsum(padded)
    starts = ends - padded
    pos1 = starts[e1] + route[:, 4].astype(jnp.int32)
    pos2 = starts[e2] + route[:, 5].astype(jnp.int32)
    n_tiles = (2 * n + n_exp * (tms - 1)) // tms
    p = n_tiles * tms
    n_act = ends[-1] // tms
    tid = jnp.minimum(jnp.arange(n_tiles, dtype=jnp.int32), n_act - 1)
    te = jnp.sum((tid[:, None] * tms >= ends[None, :]).astype(jnp.int32), axis=1)
    act = (jnp.arange(n_tiles, dtype=jnp.int32) < n_act).astype(jnp.int32)
    tok = jnp.arange(n, dtype=jnp.int32)
    inv = jnp.zeros((p,), jnp.int32).at[pos1].set(tok).at[pos2].set(tok)
    xs = jnp.take(x, inv, axis=0)
    ys = _moe_ffn(te, tid, act, xs, wg, wu, wd, tms)
    g1 = jnp.take(ys, pos1, axis=0)
    g2 = jnp.take(ys, pos2, axis=0)
    return _combine_ln(x, g1, g2, route, g, b, alpha)


def kernel(x_prompt, x_sample, state_pool, state_conv, meta, pool_w, pool_scale, conv_w1, conv_b1, conv_dw, conv_dw_b, conv_ln_g, conv_ln_b, conv_w2, conv_b2, ln_mix_g, ln_mix_b, ln_ffn_g, ln_ffn_b, ffn_w_gate, ffn_w_up, ffn_w_down, moe_router, moe_w_gate, moe_w_up, moe_w_down):
    bsz, seq, d = x_prompt.shape
    ns, ts, _ = x_sample.shape
    depth = ln_mix_g.shape[0]
    n_meta = meta.shape[0]
    pctx = state_pool.shape[2]
    cctx = state_conv.shape[2]
    assert pctx == max(POOL_WINDOWS) - 1 and cctx == conv_dw.shape[1] - 1
    alpha = float((2 * depth) ** 0.25)
    tp = seq + n_meta
    n_p = bsz * tp
    n = n_p + ns * ts

    row = lambda v: v.reshape(1, -1)
    meta_b = jnp.broadcast_to(meta[None], (bsz, n_meta, d))
    x = jnp.concatenate([meta_b, x_prompt], axis=1).reshape(n_p, d)
    x = jnp.concatenate([x, x_sample.transpose(1, 0, 2).reshape(ns * ts, d)], axis=0)

    pool_p, conv_p, pool_s, conv_s = [], [], [], []
    for i in range(depth):
        j = i // 2
        xs_t = x[n_p:].reshape(ts, ns, d)
        if i % 2 == 0:
            pool_p.append(x[:n_p].reshape(bsz, tp, d)[:, tp - pctx:])
            pool_s.append(jnp.concatenate([state_pool[j][:, ts:], xs_t.transpose(1, 0, 2)], axis=1))
            args = (pool_w[j], row(pool_scale[j]), row(ln_mix_g[i]), row(ln_mix_b[i]), alpha)
            yp = _pool_prompt(x, n, bsz, tp, *args)
            ys = _pool_sample(state_pool[j].transpose(1, 0, 2), xs_t, *args)
            x = lax.dynamic_update_slice(yp, ys.reshape(ns * ts, d), (n_p, 0))
            x = _ffn_dense(x, ffn_w_gate[j], ffn_w_up[j], ffn_w_down[j],
                           row(ln_ffn_g[i]), row(ln_ffn_b[i]), alpha)
        else:
            u = _glu(x, conv_w1[j], row(conv_b1[j]))
            us_t = u[n_p:].reshape(ts, ns, d)
            conv_p.append(u[:n_p].reshape(bsz, tp, d)[:, tp - cctx:])
            conv_s.append(jnp.concatenate([state_conv[j][:, ts:], us_t.transpose(1, 0, 2)], axis=1))
            args = (conv_dw[j], row(conv_dw_b[j]), row(conv_ln_g[j]), row(conv_ln_b[j]), conv_w2[j],
                    row(conv_b2[j]), row(ln_mix_g[i]), row(ln_mix_b[i]), alpha)
            yp = _conv_prompt(u, x, n, bsz, tp, *args)
            ys = _conv_sample(state_conv[j].transpose(1, 0, 2), us_t, xs_t, *args)
            x = lax.dynamic_update_slice(yp, ys.reshape(ns * ts, d), (n_p, 0))
            x = _moe(x, moe_router[j], moe_w_gate[j], moe_w_up[j], moe_w_down[j],
                     row(ln_ffn_g[i]), row(ln_ffn_b[i]), alpha)

    y_prompt = x[:n_p].reshape(bsz, tp, d)[:, n_meta:]
    y_sample = x[n_p:].reshape(ts, ns, d).transpose(1, 0, 2)
    return (y_prompt, y_sample, jnp.stack(pool_p), jnp.stack(conv_p), jnp.stack(pool_s), jnp.stack(conv_s))
```

```python
import functools

import jax
import jax.numpy as jnp
from jax import lax
from jax.experimental import pallas as pl
from jax.experimental.pallas import tpu as pltpu

F32 = jnp.float32
BF16 = jnp.bfloat16
I32 = jnp.int32

POOL_WINDOWS = (2, 4, 8, 16)
PAST_LEN = 16384
LN_EPS = 1e-5
LANES = 128
SUBLANES = 8
VMEM_LIMIT = 56 * 1024 * 1024
NEG = -1e30


def _cparams(sem):
    return pltpu.CompilerParams(dimension_semantics=sem, vmem_limit_bytes=VMEM_LIMIT)


def _pick(n, cands):
    for c in cands:
        if n % c == 0:
            return c
    raise ValueError(f"no tile for {n} in {cands}")


def _ln(z, g, b):
    mu = jnp.mean(z, axis=-1, keepdims=True)
    zc = z - mu
    var = jnp.mean(zc * zc, axis=-1, keepdims=True)
    return zc * lax.rsqrt(var + LN_EPS) * g + b


def _dot(a, b):
    return jnp.dot(a, b, preferred_element_type=F32)


def _pool_prompt_kernel(x_ref, w_ref, sc_ref, g_ref, b_ref, o_ref, ext_ref, *, tt, ctx, alpha):
    t = pl.program_id(1)
    d_model = x_ref.shape[-1]
    gc = d_model // len(POOL_WINDOWS)

    @pl.when(t == 0)
    def _():
        ext_ref[0:ctx, :] = jnp.zeros((ctx, d_model), F32)

    @pl.when(t > 0)
    def _():
        ext_ref[0:ctx, :] = ext_ref[tt:tt + ctx, :]

    x = x_ref[...]
    ext_ref[ctx:ctx + tt, :] = x
    pos = t * tt + lax.broadcasted_iota(I32, (tt, 1), 0)
    ys = []
    for gi, w in enumerate(POOL_WINDOWS):
        lo, hi = gi * gc, (gi + 1) * gc
        s = ext_ref[ctx:ctx + tt, lo:hi]
        for j in range(1, w):
            s = s + ext_ref[ctx - j:ctx - j + tt, lo:hi]
        cnt = jnp.minimum(pos + 1, w).astype(F32)
        dlt = s / cnt - x[:, lo:hi]
        ys.append(_dot(dlt.astype(BF16), w_ref[gi].astype(BF16)))
    y = jnp.concatenate(ys, axis=1) * sc_ref[...]
    o_ref[...] = _ln(alpha * x + y, g_ref[...], b_ref[...])


def _pool_prompt(x, n_out, bsz, tp, w, sc, g, b, alpha):
    d = x.shape[-1]
    tt = _pick(tp, (688, 512, 256, 128, 64, 32, 16))
    nt = tp // tt
    ctx = 16
    kern = functools.partial(_pool_prompt_kernel, tt=tt, ctx=ctx, alpha=alpha)
    vec = pl.BlockSpec((1, d), lambda i, t: (0, 0))
    return pl.pallas_call(
        kern,
        grid=(bsz, nt),
        in_specs=[pl.BlockSpec((tt, d), lambda i, t: (i * nt + t, 0)),
                  pl.BlockSpec(w.shape, lambda i, t: (0, 0, 0)),
                  vec, vec, vec],
        out_specs=pl.BlockSpec((tt, d), lambda i, t: (i * nt + t, 0)),
        out_shape=jax.ShapeDtypeStruct((n_out, d), F32),
        scratch_shapes=[pltpu.VMEM((tt + ctx, d), F32)],
        compiler_params=_cparams(("arbitrary", "arbitrary")),
        name="pool_prompt",
    )(x, w, sc, g, b)


def _pool_sample_kernel(ext_ref, w_ref, sc_ref, g_ref, b_ref, flat_ref, o_ref, *, ctx, alpha):
    del flat_ref
    t = pl.program_id(0)
    d_model = ext_ref.shape[-1]
    gc = d_model // len(POOL_WINDOWS)
    x = ext_ref[ctx + t]
    ys = []
    for gi, w in enumerate(POOL_WINDOWS):
        lo, hi = gi * gc, (gi + 1) * gc
        s = x[:, lo:hi]
        for j in range(1, w):
            s = s + ext_ref[ctx + t - j, :, lo:hi]
        cnt = jnp.minimum(PAST_LEN + t + 1, w).astype(F32)
        dlt = s / cnt - x[:, lo:hi]
        ys.append(_dot(dlt.astype(BF16), w_ref[gi].astype(BF16)))
    y = jnp.concatenate(ys, axis=1) * sc_ref[...]
    o_ref[...] = _ln(alpha * x + y, g_ref[...], b_ref[...])


def _pool_sample(ext, flat, n_p, ctx, w, sc, g, b, alpha):
    tot, ns, d = ext.shape
    ts = tot - ctx
    assert n_p % ns == 0
    kern = functools.partial(_pool_sample_kernel, ctx=ctx, alpha=alpha)
    vec = pl.BlockSpec((1, d), lambda t: (0, 0))
    return pl.pallas_call(
        kern,
        grid=(ts,),
        in_specs=[pl.BlockSpec(ext.shape, lambda t: (0, 0, 0)),
                  pl.BlockSpec(w.shape, lambda t: (0, 0, 0)),
                  vec, vec, vec,
                  pl.BlockSpec(memory_space=pl.ANY)],
        out_specs=pl.BlockSpec((ns, d), lambda t: (n_p // ns + t, 0)),
        out_shape=jax.ShapeDtypeStruct(flat.shape, F32),
        input_output_aliases={5: 0},
        compiler_params=_cparams(("arbitrary",)),
        name="pool_sample",
    )(ext, w, sc, g, b, flat)


def _swiglu_part(xb, wg_ref, wu_ref, wd_ref):
    hg = _dot(xb, wg_ref[...].astype(BF16))
    hu = _dot(xb, wu_ref[...].astype(BF16))
    a = (hg * jax.nn.sigmoid(hg) * hu).astype(BF16)
    return _dot(a, wd_ref[...].astype(BF16))


def _ffn_ln_kernel(x_ref, wg_ref, wu_ref, wd_ref, g_ref, b_ref, o_ref, xb_ref, acc_ref, *, alpha):
    f = pl.program_id(1)
    nf = pl.num_programs(1)

    @pl.when(f == 0)
    def _():
        xb_ref[...] = x_ref[...].astype(BF16)

    part = _swiglu_part(xb_ref[...], wg_ref, wu_ref, wd_ref)

    @pl.when(f == 0)
    def _():
        acc_ref[...] = part

    @pl.when(f > 0)
    def _():
        acc_ref[...] += part

    @pl.when(f == nf - 1)
    def _():
        o_ref[...] = _ln(alpha * x_ref[...] + acc_ref[...], g_ref[...], b_ref[...])


def _ffn_dense(x, wg, wu, wd, g, b, alpha):
    n, d = x.shape
    ff = wg.shape[1]
    tm = _pick(n, (896, 512, 256, 128, 64, 32, 16, 8))
    fc = _pick(ff, (256, 128))
    kern = functools.partial(_ffn_ln_kernel, alpha=alpha)
    vec = pl.BlockSpec((1, d), lambda i, f: (0, 0))
    return pl.pallas_call(
        kern,
        grid=(n // tm, ff // fc),
        in_specs=[pl.BlockSpec((tm, d), lambda i, f: (i, 0)),
                  pl.BlockSpec((d, fc), lambda i, f: (0, f)),
                  pl.BlockSpec((d, fc), lambda i, f: (0, f)),
                  pl.BlockSpec((fc, d), lambda i, f: (f, 0)),
                  vec, vec],
        out_specs=pl.BlockSpec((tm, d), lambda i, f: (i, 0)),
        out_shape=jax.ShapeDtypeStruct((n, d), F32),
        scratch_shapes=[pltpu.VMEM((tm, d), BF16), pltpu.VMEM((tm, d), F32)],
        compiler_params=_cparams(("arbitrary", "arbitrary")),
        name="ffn_dense",
    )(x, wg, wu, wd, g, b)


def _moe_ffn_kernel(te_ref, rb_ref, act_ref, x_ref, wg_ref, wu_ref, wd_ref, o_ref, xb_ref):
    t = pl.program_id(0)
    f = pl.program_id(1)

    @pl.when(act_ref[t] == 1)
    def _():
        @pl.when(f == 0)
        def _():
            xb_ref[...] = x_ref[...].astype(BF16)

        part = _swiglu_part(xb_ref[...], wg_ref, wu_ref, wd_ref)

        @pl.when(f == 0)
        def _():
            o_ref[...] = part

        @pl.when(f > 0)
        def _():
            o_ref[...] += part


def _moe_ffn(te, rb, act, xs, wg, wu, wd, tms):
    p, d = xs.shape
    ff = wg.shape[2]
    fc = _pick(ff, (256, 128))
    nf = ff // fc

    def fidx(t, f, act):
        return jnp.where(act[t] == 1, f, nf - 1)

    grid_spec = pltpu.PrefetchScalarGridSpec(
        num_scalar_prefetch=3,
        grid=(p // tms, nf),
        in_specs=[pl.BlockSpec((tms, d), lambda t, f, te, rb, act: (rb[t], 0)),
                  pl.BlockSpec((None, d, fc), lambda t, f, te, rb, act: (te[t], 0, fidx(t, f, act))),
                  pl.BlockSpec((None, d, fc), lambda t, f, te, rb, act: (te[t], 0, fidx(t, f, act))),
                  pl.BlockSpec((None, fc, d), lambda t, f, te, rb, act: (te[t], fidx(t, f, act), 0))],
        out_specs=pl.BlockSpec((tms, d), lambda t, f, te, rb, act: (rb[t], 0)),
        scratch_shapes=[pltpu.VMEM((tms, d), BF16)],
    )
    return pl.pallas_call(
        _moe_ffn_kernel,
        grid_spec=grid_spec,
        out_shape=jax.ShapeDtypeStruct((p, d), F32),
        compiler_params=_cparams(("arbitrary", "arbitrary")),
        name="moe_ffn",
    )(te, rb, act, xs, wg, wu, wd)


def _glu_kernel(x_ref, w_ref, b_ref, o_ref, wb_ref):
    d_model = x_ref.shape[-1]

    @pl.when(pl.program_id(0) == 0)
    def _():
        wb_ref[...] = w_ref[...].astype(BF16)

    xb = x_ref[...].astype(BF16)
    cw = min(256, d_model)
    for j in range(d_model // cw):
        lo, hi = j * cw, (j + 1) * cw
        a = _dot(xb, wb_ref[:, lo:hi]) + b_ref[:, lo:hi]
        gt = _dot(xb, wb_ref[:, d_model + lo:d_model + hi]) + b_ref[:, d_model + lo:d_model + hi]
        o_ref[:, lo:hi] = a * jax.nn.sigmoid(gt)


def _glu(x, w1, b1):
    n, d = x.shape
    tm = _pick(n, (896, 512, 256, 128, 64, 32, 16, 8))
    return pl.pallas_call(
        _glu_kernel,
        grid=(n // tm,),
        in_specs=[pl.BlockSpec((tm, d), lambda i: (i, 0)),
                  pl.BlockSpec((d, 2 * d), lambda i: (0, 0)),
                  pl.BlockSpec((1, 2 * d), lambda i: (0, 0))],
        out_specs=pl.BlockSpec((tm, d), lambda i: (i, 0)),
        out_shape=jax.ShapeDtypeStruct((n, d), F32),
        scratch_shapes=[pltpu.VMEM((d, 2 * d), BF16)],
        compiler_params=_cparams(("arbitrary",)),
        name="conv_glu",
    )(x, w1, b1)


def _conv_tail(c, x, dwb, lng, lnb, w2b, b2, g, b, alpha):
    c = _ln(c + dwb, lng, lnb)
    s = (c * jax.nn.sigmoid(c)).astype(BF16)
    y = _dot(s, w2b) + b2
    return _ln(alpha * x + y, g, b)


def _conv_chunk(ext_ref, dw_ref, c_ref, base, rc, off, lw):
    kw, d_model = dw_ref.shape
    for lo in range(0, d_model, lw):
        acc = None
        for b in range(SUBLANES):
            vb = None
            for a in range((kw + off) // SUBLANES + 1):
                k = SUBLANES * a + b - off
                if 0 <= k < kw:
                    term = (ext_ref[pl.ds(base + SUBLANES * a, rc + SUBLANES), lo:lo + lw]
                            * dw_ref[k:k + 1, lo:lo + lw])
                    vb = term if vb is None else vb + term
            if vb is not None:
                sh = vb[b:b + rc]
                acc = sh if acc is None else acc + sh
        c_ref[pl.ds(base, rc), lo:lo + lw] = acc


def _conv_prompt_kernel(u_ref, x_ref, dw_ref, dwb_ref, lng_ref, lnb_ref, w2_ref, b2_ref, g_ref, b_ref,
                        o_ref, ext_ref, c_ref, w2b_ref, *, tt, ctx, rc, alpha):
    i = pl.program_id(0)
    t = pl.program_id(1)
    d_model = x_ref.shape[-1]
    kw = dw_ref.shape[0]
    off = ctx - (kw - 1)
    lw = min(256, d_model)

    @pl.when((i == 0) & (t == 0))
    def _():
        w2b_ref[...] = w2_ref[...].astype(BF16)
        ext_ref[ctx + tt:ctx + tt + SUBLANES, :] = jnp.zeros((SUBLANES, d_model), F32)

    @pl.when(t == 0)
    def _():
        ext_ref[0:ctx, :] = jnp.zeros((ctx, d_model), F32)

    @pl.when(t > 0)
    def _():
        ext_ref[0:ctx, :] = ext_ref[tt:tt + ctx, :]

    ext_ref[ctx:ctx + tt, :] = u_ref[...]

    n_full = tt // rc

    def body(ci, carry):
        _conv_chunk(ext_ref, dw_ref, c_ref, pl.multiple_of(ci * rc, SUBLANES), rc, off, lw)
        return carry

    lax.fori_loop(0, n_full, body, 0)
    if tt % rc:
        _conv_chunk(ext_ref, dw_ref, c_ref, n_full * rc, tt % rc, off, lw)

    o_ref[...] = _conv_tail(c_ref[...], x_ref[...], dwb_ref[...], lng_ref[...], lnb_ref[...],
                            w2b_ref[...], b2_ref[...], g_ref[...], b_ref[...], alpha)


def _conv_prompt(u, x, n_out, bsz, tp, dw, dwb, lng, lnb, w2, b2, g, b, alpha):
    d = x.shape[-1]
    tt = _pick(tp, (688, 512, 256, 128, 64, 32))
    nt = tp // tt
    ctx = 32
    rc = min(64, tt)
    assert dw.shape[0] - 1 <= ctx and tt % SUBLANES == 0
    kern = functools.partial(_conv_prompt_kernel, tt=tt, ctx=ctx, rc=rc, alpha=alpha)
    vec = pl.BlockSpec((1, d), lambda i, t: (0, 0))
    tile = pl.BlockSpec((tt, d), lambda i, t: (i * nt + t, 0))
    return pl.pallas_call(
        kern,
        grid=(bsz, nt),
        in_specs=[tile, tile,
                  pl.BlockSpec(dw.shape, lambda i, t: (0, 0)),
                  vec, vec, vec,
                  pl.BlockSpec((d, d), lambda i, t: (0, 0)),
                  vec, vec, vec],
        out_specs=tile,
        out_shape=jax.ShapeDtypeStruct((n_out, d), F32),
        scratch_shapes=[pltpu.VMEM((tt + ctx + SUBLANES, d), F32), pltpu.VMEM((tt, d), F32),
                        pltpu.VMEM((d, d), BF16)],
        compiler_params=_cparams(("arbitrary", "arbitrary")),
        name="conv_prompt",
    )(u, x, dw, dwb, lng, lnb, w2, b2, g, b)


def _conv_sample_kernel(ext_ref, x_ref, dw_ref, dwb_ref, lng_ref, lnb_ref, w2_ref, b2_ref,
                        g_ref, b_ref, flat_ref, o_ref, *, alpha):
    del flat_ref
    t = pl.program_id(0)
    kw = dw_ref.shape[0]
    acc = ext_ref[t] * dw_ref[0:1, :]
    for k in range(1, kw):
        acc = acc + ext_ref[t + k] * dw_ref[k:k + 1, :]
    o_ref[...] = _conv_tail(acc, x_ref[...], dwb_ref[...], lng_ref[...], lnb_ref[...],
                            w2_ref[...].astype(BF16), b2_ref[...], g_ref[...], b_ref[...], alpha)


def _conv_sample(ext, x, flat, n_p, dw, dwb, lng, lnb, w2, b2, g, b, alpha):
    tot, ns, d = ext.shape
    ts = tot - (dw.shape[0] - 1)
    assert n_p % ns == 0
    kern = functools.partial(_conv_sample_kernel, alpha=alpha)
    vec = pl.BlockSpec((1, d), lambda t: (0, 0))
    rows = pl.BlockSpec((ns, d), lambda t: (n_p // ns + t, 0))
    return pl.pallas_call(
        kern,
        grid=(ts,),
        in_specs=[pl.BlockSpec(ext.shape, lambda t: (0, 0, 0)), rows,
                  pl.BlockSpec(dw.shape, lambda t: (0, 0)),
                  vec, vec, vec,
                  pl.BlockSpec((d, d), lambda t: (0, 0)),
                  vec, vec, vec,
                  pl.BlockSpec(memory_space=pl.ANY)],
        out_specs=rows,
        out_shape=jax.ShapeDtypeStruct(flat.shape, F32),
        input_output_aliases={10: 0},
        compiler_params=_cparams(("arbitrary",)),
        name="conv_sample",
    )(ext, x, dw, dwb, lng, lnb, w2, b2, g, b, flat)


def _route_kernel(x_ref, r_ref, route_ref, cnt_ref, tri_ref, carry_ref, *, n_exp):
    i = pl.program_id(0)
    tm = x_ref.shape[0]

    @pl.when(i == 0)
    def _():
        carry_ref[...] = jnp.zeros(carry_ref.shape, F32)
        rr = lax.broadcasted_iota(I32, (tm, tm), 0)
        cc = lax.broadcasted_iota(I32, (tm, tm), 1)
        tri_ref[...] = jnp.where(rr > cc, 1.0, 0.0).astype(BF16)

    x = x_ref[...]
    r = r_ref[...]
    xh = x.astype(BF16)
    xl = (x - xh.astype(F32)).astype(BF16)
    rh = r.astype(BF16)
    rl = (r - rh.astype(F32)).astype(BF16)
    logits = _dot(xh, rh) + (_dot(xl, rh) + _dot(xh, rl))
    lane = lax.broadcasted_iota(I32, (tm, LANES), 1).astype(F32)
    l1 = jnp.where(lane < n_exp, logits, NEG)
    m1 = jnp.max(l1, axis=-1, keepdims=True)
    i1 = jnp.min(jnp.where(l1 == m1, lane, float(LANES)), axis=-1, keepdims=True)
    l2 = jnp.where(lane == i1, NEG, l1)
    m2 = jnp.max(l2, axis=-1, keepdims=True)
    i2 = jnp.min(jnp.where(l2 == m2, lane, float(LANES)), axis=-1, keepdims=True)
    ex = jnp.exp(m2 - m1)
    p1 = 1.0 / (1.0 + ex)
    p2 = ex * p1
    oh1 = lane == i1
    oh2 = lane == i2
    m = jnp.where(oh1 | oh2, 1.0, 0.0)
    before = _dot(tri_ref[...], m.astype(BF16)) + carry_ref[...]
    rank1 = jnp.sum(jnp.where(oh1, before, 0.0), axis=-1, keepdims=True)
    rank2 = jnp.sum(jnp.where(oh2, before, 0.0), axis=-1, keepdims=True)
    carry_ref[...] += jnp.sum(m, axis=0, keepdims=True)
    route = jnp.where(lane == 0, i1, 0.0)
    for k, v in enumerate((i2, p1, p2, rank1, rank2), start=1):
        route = jnp.where(lane == k, v, route)
    route_ref[...] = route
    cnt_ref[...] = carry_ref[...]


def _route(x, router):
    n, d = x.shape
    n_exp = router.shape[1]
    tm = _pick(n, (896, 512, 256, 128, 64, 32, 16, 8))
    rp = jnp.pad(router, ((0, 0), (0, LANES - n_exp)))
    kern = functools.partial(_route_kernel, n_exp=n_exp)
    return pl.pallas_call(
        kern,
        grid=(n // tm,),
        in_specs=[pl.BlockSpec((tm, d), lambda i: (i, 0)),
                  pl.BlockSpec((d, LANES), lambda i: (0, 0))],
        out_specs=[pl.BlockSpec((tm, LANES), lambda i: (i, 0)),
                   pl.BlockSpec((1, LANES), lambda i: (0, 0))],
        out_shape=[jax.ShapeDtypeStruct((n, LANES), F32), jax.ShapeDtypeStruct((1, LANES), F32)],
        scratch_shapes=[pltpu.VMEM((tm, tm), BF16), pltpu.VMEM((1, LANES), F32)],
        compiler_params=_cparams(("arbitrary",)),
        name="route",
    )(x, rp)


def _invert_kernel(p1_ref, p2_ref, st_ref, cn_ref, pd_ref, src_ref, dst_ref, *, n, tms):
    n_exp = st_ref.shape[0]
    for e in range(n_exp):
        def pad_body(s, carry):
            src_ref[s] = 0
            dst_ref[s] = 2 * n + lax.rem(s, tms)
            return carry

        lax.fori_loop(st_ref[e] + cn_ref[e], st_ref[e] + pd_ref[e], pad_body, 0)

    unroll = SUBLANES if n % SUBLANES == 0 else 1

    def body(c, carry):
        toks = [c * unroll + j for j in range(unroll)]
        slots = [(p1_ref[i], p2_ref[i]) for i in toks]
        for i, (a, b) in zip(toks, slots):
            src_ref[a] = i
            dst_ref[a] = i
            src_ref[b] = i
            dst_ref[b] = n + i
        return carry

    lax.fori_loop(0, n // unroll, body, 0)


def _invert(pos1, pos2, starts, counts, padded, p, tms):
    n = pos1.shape[0]
    smem = pl.BlockSpec(memory_space=pltpu.SMEM)
    kern = functools.partial(_invert_kernel, n=n, tms=tms)
    return pl.pallas_call(
        kern,
        in_specs=[smem] * 5,
        out_specs=[smem, smem],
        out_shape=[jax.ShapeDtypeStruct((p,), I32), jax.ShapeDtypeStruct((p,), I32)],
        name="invert",
    )(pos1, pos2, starts, counts, padded)


def _row_copy_wait(src, dst, sem):
    pltpu.make_async_copy(src, dst, sem).wait()


def _gather_rows_kernel(rb_ref, act_ref, src_ref, x_hbm, o_ref, sem, *, tms):
    t = pl.program_id(0)
    unroll = SUBLANES

    @pl.when(act_ref[t] == 1)
    def _():
        base = t * tms

        def body(i, carry):
            for j in range(unroll):
                r = i * unroll + j
                pltpu.make_async_copy(x_hbm.at[pl.ds(src_ref[base + r], 1)],
                                      o_ref.at[pl.ds(r, 1)], sem).start()
            return carry

        lax.fori_loop(0, tms // unroll, body, 0)
        _row_copy_wait(x_hbm.at[pl.ds(0, tms)], o_ref, sem)


def _gather_rows(rb, act, src, x, p, tms):
    d = x.shape[1]
    grid_spec = pltpu.PrefetchScalarGridSpec(
        num_scalar_prefetch=3,
        grid=(p // tms,),
        in_specs=[pl.BlockSpec(memory_space=pl.ANY)],
        out_specs=pl.BlockSpec((tms, d), lambda t, rb, act, src: (rb[t], 0)),
        scratch_shapes=[pltpu.SemaphoreType.DMA(())],
    )
    return pl.pallas_call(
        functools.partial(_gather_rows_kernel, tms=tms),
        grid_spec=grid_spec,
        out_shape=jax.ShapeDtypeStruct((p, d), F32),
        compiler_params=_cparams(("arbitrary",)),
        name="gather_rows",
    )(rb, act, src, x)


def _scatter_rows_kernel(rb_ref, act_ref, dst_ref, ys_ref, y2_hbm, sem, *, tms):
    t = pl.program_id(0)
    unroll = SUBLANES

    @pl.when(act_ref[t] == 1)
    def _():
        base = t * tms

        def body(i, carry):
            for j in range(unroll):
                r = i * unroll + j
                pltpu.make_async_copy(ys_ref.at[pl.ds(r, 1)],
                                      y2_hbm.at[pl.ds(dst_ref[base + r], 1)], sem).start()
            return carry

        lax.fori_loop(0, tms // unroll, body, 0)
        _row_copy_wait(ys_ref, y2_hbm.at[pl.ds(0, tms)], sem)


def _scatter_rows(rb, act, dst, ys, n_rows, tms):
    p, d = ys.shape
    grid_spec = pltpu.PrefetchScalarGridSpec(
        num_scalar_prefetch=3,
        grid=(p // tms,),
        in_specs=[pl.BlockSpec((tms, d), lambda t, rb, act, dst: (rb[t], 0))],
        out_specs=pl.BlockSpec(memory_space=pl.ANY),
        scratch_shapes=[pltpu.SemaphoreType.DMA(())],
    )
    return pl.pallas_call(
        functools.partial(_scatter_rows_kernel, tms=tms),
        grid_spec=grid_spec,
        out_shape=jax.ShapeDtypeStruct((n_rows, d), F32),
        compiler_params=_cparams(("arbitrary",)),
        name="scatter_rows",
    )(rb, act, dst, ys)


def _combine_ln_kernel(x_ref, g1_ref, g2_ref, route_ref, g_ref, b_ref, o_ref, *, alpha):
    p1 = route_ref[:, 2:3]
    p2 = route_ref[:, 3:4]
    y = p1 * g1_ref[...] + p2 * g2_ref[...]
    o_ref[...] = _ln(alpha * x_ref[...] + y, g_ref[...], b_ref[...])


def _combine_ln(x, y2, route, g, b, alpha):
    n, d = x.shape
    tm = _pick(n, (896, 512, 256, 128, 64, 32, 16, 8))
    nb = n // tm
    kern = functools.partial(_combine_ln_kernel, alpha=alpha)
    tile = pl.BlockSpec((tm, d), lambda i: (i, 0))
    vec = pl.BlockSpec((1, d), lambda i: (0, 0))
    return pl.pallas_call(
        kern,
        grid=(nb,),
        in_specs=[tile, tile, pl.BlockSpec((tm, d), lambda i: (nb + i, 0)),
                  pl.BlockSpec((tm, LANES), lambda i: (i, 0)), vec, vec],
        out_specs=tile,
        out_shape=jax.ShapeDtypeStruct((n, d), F32),
        compiler_params=_cparams(("arbitrary",)),
        name="combine_ln",
    )(x, y2, y2, route, g, b)


def _moe(x, router, wg, wu, wd, g, b, alpha):
    n, d = x.shape
    n_exp = router.shape[1]
    tms = 1024 if n >= 4096 else 32
    route, cnt = _route(x, router)
    e1 = route[:, 0].astype(I32)
    e2 = route[:, 1].astype(I32)
    counts = cnt[0, :n_exp].astype(I32)
    padded = ((counts + tms - 1) // tms) * tms
    ends = jnp.cumsum(padded)
    starts = ends - padded
    pos1 = starts[e1] + route[:, 4].astype(I32)
    pos2 = starts[e2] + route[:, 5].astype(I32)
    n_tiles = (2 * n + n_exp * (tms - 1)) // tms
    p = n_tiles * tms
    n_act = ends[-1] // tms
    tiles = jnp.arange(n_tiles, dtype=I32)
    rb = jnp.minimum(tiles, n_act - 1)
    te = jnp.sum((rb[:, None] * tms >= ends[None, :]).astype(I32), axis=1)
    act = (tiles < n_act).astype(I32)
    src, dst = _invert(pos1, pos2, starts, counts, padded, p, tms)
    xs = _gather_rows(rb, act, src, x, p, tms)
    ys = _moe_ffn(te, rb, act, xs, wg, wu, wd, tms)
    y2 = _scatter_rows(rb, act, dst, ys, 2 * n + tms, tms)
    return _combine_ln(x, y2, route, g, b, alpha)


def _tail_rows(x, bsz, tp, k):
    return jnp.stack([x[(i + 1) * tp - k:(i + 1) * tp] for i in range(bsz)])


def kernel(x_prompt, x_sample, state_pool, state_conv, meta, pool_w, pool_scale, conv_w1, conv_b1, conv_dw, conv_dw_b, conv_ln_g, conv_ln_b, conv_w2, conv_b2, ln_mix_g, ln_mix_b, ln_ffn_g, ln_ffn_b, ffn_w_gate, ffn_w_up, ffn_w_down, moe_router, moe_w_gate, moe_w_up, moe_w_down):
    bsz, seq, d = x_prompt.shape
    ns, ts, _ = x_sample.shape
    depth = ln_mix_g.shape[0]
    n_meta = meta.shape[0]
    pctx = state_pool.shape[2]
    cctx = state_conv.shape[2]
    assert pctx == max(POOL_WINDOWS) - 1 and cctx == conv_dw.shape[1] - 1
    alpha = float((2 * depth) ** 0.25)
    tp = seq + n_meta
    n_p = bsz * tp
    n = n_p + ns * ts

    row = lambda v: v.reshape(1, -1)
    meta_b = jnp.broadcast_to(meta[None], (bsz, n_meta, d))
    x = jnp.concatenate([meta_b, x_prompt], axis=1).reshape(n_p, d)
    x = jnp.concatenate([x, x_sample.transpose(1, 0, 2).reshape(ns * ts, d)], axis=0)

    pool_p, conv_p, pool_s, conv_s = [], [], [], []
    for i in range(depth):
        j = i // 2
        xs_t = x[n_p:].reshape(ts, ns, d)
        if i % 2 == 0:
            pool_p.append(_tail_rows(x, bsz, tp, pctx))
            pool_s.append(jnp.concatenate([state_pool[j][:, ts:], xs_t.transpose(1, 0, 2)], axis=1))
            args = (pool_w[j], row(pool_scale[j]), row(ln_mix_g[i]), row(ln_mix_b[i]), alpha)
            ext = jnp.concatenate([state_pool[j].transpose(1, 0, 2), xs_t], axis=0)
            y = _pool_prompt(x, n, bsz, tp, *args)
            x = _pool_sample(ext, y, n_p, pctx, *args)
            x = _ffn_dense(x, ffn_w_gate[j], ffn_w_up[j], ffn_w_down[j],
                           row(ln_ffn_g[i]), row(ln_ffn_b[i]), alpha)
        else:
            u = _glu(x, conv_w1[j], row(conv_b1[j]))
            us_t = u[n_p:].reshape(ts, ns, d)
            conv_p.append(_tail_rows(u, bsz, tp, cctx))
            conv_s.append(jnp.concatenate([state_conv[j][:, ts:], us_t.transpose(1, 0, 2)], axis=1))
            args = (conv_dw[j], row(conv_dw_b[j]), row(conv_ln_g[j]), row(conv_ln_b[j]), conv_w2[j],
                    row(conv_b2[j]), row(ln_mix_g[i]), row(ln_mix_b[i]), alpha)
            ext = jnp.concatenate([state_conv[j].transpose(1, 0, 2), us_t], axis=0)
            y = _conv_prompt(u, x, n, bsz, tp, *args)
            x = _conv_sample(ext, x, y, n_p, *args)
            x = _moe(x, moe_router[j], moe_w_gate[j], moe_w_up[j], moe_w_down[j],
                     row(ln_ffn_g[i]), row(ln_ffn_b[i]), alpha)

    y_prompt = x[:n_p].reshape(bsz, tp, d)[:, n_meta:]
    y_sample = x[n_p:].reshape(ts, ns, d).transpose(1, 0, 2)
    return (y_prompt, y_sample, jnp.stack(pool_p), jnp.stack(conv_p), jnp.stack(pool_s), jnp.stack(conv_s))
```

```python
import functools

import jax
import jax.numpy as jnp
from jax import lax
from jax.experimental import pallas as pl
from jax.experimental.pallas import tpu as pltpu

F32 = jnp.float32
BF16 = jnp.bfloat16
I32 = jnp.int32

POOL_WINDOWS = (2, 4, 8, 16)
PAST_LEN = 16384
LN_EPS = 1e-5
LANES = 128
SUBLANES = 8
VMEM_LIMIT = 56 * 1024 * 1024
NEG = -1e30


def _cparams(sem):
    return pltpu.CompilerParams(dimension_semantics=sem, vmem_limit_bytes=VMEM_LIMIT)


def _pick(n, cands):
    for c in cands:
        if n % c == 0:
            return c
    raise ValueError(f"no tile for {n} in {cands}")


def _whole(a):
    nd = a.ndim
    return pl.BlockSpec(a.shape, lambda *_: (0,) * nd)


def _layer(a, j):
    nd = a.ndim
    return pl.BlockSpec((None,) + a.shape[1:], lambda *_: (j,) + (0,) * (nd - 1))


def _row(ref, k):
    return ref[k:k + 1, :]


def _ln(z, g, b):
    mu = jnp.mean(z, axis=-1, keepdims=True)
    zc = z - mu
    var = jnp.mean(zc * zc, axis=-1, keepdims=True)
    return zc * lax.rsqrt(var + LN_EPS) * g + b


def _dot(a, b):
    return jnp.dot(a, b, preferred_element_type=F32)


def _pool_prompt_kernel(x_ref, w_ref, sc_ref, g_ref, b_ref, o_ref, ext_ref, *, tt, ctx, alpha, j, li):
    t = pl.program_id(1)
    d_model = x_ref.shape[-1]
    gc = d_model // len(POOL_WINDOWS)

    @pl.when(t == 0)
    def _():
        ext_ref[0:ctx, :] = jnp.zeros((ctx, d_model), F32)

    @pl.when(t > 0)
    def _():
        ext_ref[0:ctx, :] = ext_ref[tt:tt + ctx, :]

    x = x_ref[...]
    ext_ref[ctx:ctx + tt, :] = x
    pos = t * tt + lax.broadcasted_iota(I32, (tt, 1), 0)
    ys = []
    for gi, w in enumerate(POOL_WINDOWS):
        lo, hi = gi * gc, (gi + 1) * gc
        s = ext_ref[ctx:ctx + tt, lo:hi]
        for k in range(1, w):
            s = s + ext_ref[ctx - k:ctx - k + tt, lo:hi]
        cnt = jnp.minimum(pos + 1, w).astype(F32)
        dlt = s / cnt - x[:, lo:hi]
        ys.append(_dot(dlt.astype(BF16), w_ref[gi].astype(BF16)))
    y = jnp.concatenate(ys, axis=1) * _row(sc_ref, j)
    o_ref[...] = _ln(alpha * x + y, _row(g_ref, li), _row(b_ref, li))


def _pool_prompt(x, n_out, bsz, tp, w, sc, g, b, alpha, j, li):
    d = x.shape[-1]
    tt = _pick(tp, (688, 512, 256, 128, 64, 32, 16))
    nt = tp // tt
    ctx = 16
    kern = functools.partial(_pool_prompt_kernel, tt=tt, ctx=ctx, alpha=alpha, j=j, li=li)
    return pl.pallas_call(
        kern,
        grid=(bsz, nt),
        in_specs=[pl.BlockSpec((tt, d), lambda i, t: (i * nt + t, 0)),
                  _layer(w, j), _whole(sc), _whole(g), _whole(b)],
        out_specs=pl.BlockSpec((tt, d), lambda i, t: (i * nt + t, 0)),
        out_shape=jax.ShapeDtypeStruct((n_out, d), F32),
        scratch_shapes=[pltpu.VMEM((tt + ctx, d), F32)],
        compiler_params=_cparams(("arbitrary", "arbitrary")),
        name="pool_prompt",
    )(x, w, sc, g, b)


def _pool_sample_kernel(ext_ref, w_ref, sc_ref, g_ref, b_ref, flat_ref, o_ref, *, ctx, alpha, j, li):
    del flat_ref
    t = pl.program_id(0)
    d_model = ext_ref.shape[-1]
    gc = d_model // len(POOL_WINDOWS)
    x = ext_ref[ctx + t]
    ys = []
    for gi, w in enumerate(POOL_WINDOWS):
        lo, hi = gi * gc, (gi + 1) * gc
        s = x[:, lo:hi]
        for k in range(1, w):
            s = s + ext_ref[ctx + t - k, :, lo:hi]
        cnt = jnp.minimum(PAST_LEN + t + 1, w).astype(F32)
        dlt = s / cnt - x[:, lo:hi]
        ys.append(_dot(dlt.astype(BF16), w_ref[gi].astype(BF16)))
    y = jnp.concatenate(ys, axis=1) * _row(sc_ref, j)
    o_ref[...] = _ln(alpha * x + y, _row(g_ref, li), _row(b_ref, li))


def _pool_sample(ext, flat, n_p, ctx, w, sc, g, b, alpha, j, li):
    tot, ns, d = ext.shape
    ts = tot - ctx
    assert n_p % ns == 0
    kern = functools.partial(_pool_sample_kernel, ctx=ctx, alpha=alpha, j=j, li=li)
    return pl.pallas_call(
        kern,
        grid=(ts,),
        in_specs=[_whole(ext), _layer(w, j), _whole(sc), _whole(g), _whole(b),
                  pl.BlockSpec(memory_space=pl.ANY)],
        out_specs=pl.BlockSpec((ns, d), lambda t: (n_p // ns + t, 0)),
        out_shape=jax.ShapeDtypeStruct(flat.shape, F32),
        input_output_aliases={5: 0},
        compiler_params=_cparams(("arbitrary",)),
        name="pool_sample",
    )(ext, w, sc, g, b, flat)


def _swiglu_part(xb, wg_ref, wu_ref, wd_ref):
    hg = _dot(xb, wg_ref[...].astype(BF16))
    hu = _dot(xb, wu_ref[...].astype(BF16))
    a = (hg * jax.nn.sigmoid(hg) * hu).astype(BF16)
    return _dot(a, wd_ref[...].astype(BF16))


def _ffn_ln_kernel(x_ref, wg_ref, wu_ref, wd_ref, g_ref, b_ref, o_ref, xb_ref, acc_ref, *, alpha, li):
    f = pl.program_id(1)
    nf = pl.num_programs(1)

    @pl.when(f == 0)
    def _():
        xb_ref[...] = x_ref[...].astype(BF16)
        acc_ref[...] = jnp.zeros(acc_ref.shape, F32)

    acc_ref[...] += _swiglu_part(xb_ref[...], wg_ref, wu_ref, wd_ref)

    @pl.when(f == nf - 1)
    def _():
        o_ref[...] = _ln(alpha * x_ref[...] + acc_ref[...], _row(g_ref, li), _row(b_ref, li))


def _ffn_dense(x, wg, wu, wd, g, b, alpha, j, li):
    n, d = x.shape
    ff = wg.shape[2]
    tm = _pick(n, (896, 512, 256, 128, 64, 32, 16, 8))
    fc = _pick(ff, (256, 128))
    kern = functools.partial(_ffn_ln_kernel, alpha=alpha, li=li)
    return pl.pallas_call(
        kern,
        grid=(n // tm, ff // fc),
        in_specs=[pl.BlockSpec((tm, d), lambda i, f: (i, 0)),
                  pl.BlockSpec((None, d, fc), lambda i, f: (j, 0, f)),
                  pl.BlockSpec((None, d, fc), lambda i, f: (j, 0, f)),
                  pl.BlockSpec((None, fc, d), lambda i, f: (j, f, 0)),
                  _whole(g), _whole(b)],
        out_specs=pl.BlockSpec((tm, d), lambda i, f: (i, 0)),
        out_shape=jax.ShapeDtypeStruct((n, d), F32),
        scratch_shapes=[pltpu.VMEM((tm, d), BF16), pltpu.VMEM((tm, d), F32)],
        compiler_params=_cparams(("arbitrary", "arbitrary")),
        name="ffn_dense",
    )(x, wg, wu, wd, g, b)


def _moe_ffn_kernel(te_ref, rb_ref, act_ref, x_ref, wg_ref, wu_ref, wd_ref, o_ref, xb_ref):
    t = pl.program_id(0)
    f = pl.program_id(1)

    @pl.when(act_ref[t] == 1)
    def _():
        @pl.when(f == 0)
        def _():
            xb_ref[...] = x_ref[...].astype(BF16)
            o_ref[...] = jnp.zeros(o_ref.shape, F32)

        o_ref[...] += _swiglu_part(xb_ref[...], wg_ref, wu_ref, wd_ref)


def _moe_ffn(te, rb, act, xs, wg, wu, wd, tms, j):
    p, d = xs.shape
    ff = wg.shape[3]
    fc = _pick(ff, (256, 128))
    nf = ff // fc

    def fidx(t, f, act):
        return jnp.where(act[t] == 1, f, nf - 1)

    grid_spec = pltpu.PrefetchScalarGridSpec(
        num_scalar_prefetch=3,
        grid=(p // tms, nf),
        in_specs=[pl.BlockSpec((tms, d), lambda t, f, te, rb, act: (rb[t], 0)),
                  pl.BlockSpec((None, None, d, fc),
                               lambda t, f, te, rb, act: (j, te[t], 0, fidx(t, f, act))),
                  pl.BlockSpec((None, None, d, fc),
                               lambda t, f, te, rb, act: (j, te[t], 0, fidx(t, f, act))),
                  pl.BlockSpec((None, None, fc, d),
                               lambda t, f, te, rb, act: (j, te[t], fidx(t, f, act), 0))],
        out_specs=pl.BlockSpec((tms, d), lambda t, f, te, rb, act: (rb[t], 0)),
        scratch_shapes=[pltpu.VMEM((tms, d), BF16)],
    )
    return pl.pallas_call(
        _moe_ffn_kernel,
        grid_spec=grid_spec,
        out_shape=jax.ShapeDtypeStruct((p, d), F32),
        compiler_params=_cparams(("arbitrary", "arbitrary")),
        name="moe_ffn",
    )(te, rb, act, xs, wg, wu, wd)


def _glu_kernel(x_ref, w_ref, b_ref, o_ref, wb_ref, *, j):
    d_model = x_ref.shape[-1]

    @pl.when(pl.program_id(0) == 0)
    def _():
        wb_ref[...] = w_ref[...].astype(BF16)

    xb = x_ref[...].astype(BF16)
    cw = min(256, d_model)
    for c in range(d_model // cw):
        lo, hi = c * cw, (c + 1) * cw
        a = _dot(xb, wb_ref[:, lo:hi]) + b_ref[j:j + 1, lo:hi]
        gt = _dot(xb, wb_ref[:, d_model + lo:d_model + hi]) + b_ref[j:j + 1, d_model + lo:d_model + hi]
        o_ref[:, lo:hi] = a * jax.nn.sigmoid(gt)


def _glu(x, w1, b1, j):
    n, d = x.shape
    tm = _pick(n, (896, 512, 256, 128, 64, 32, 16, 8))
    return pl.pallas_call(
        functools.partial(_glu_kernel, j=j),
        grid=(n // tm,),
        in_specs=[pl.BlockSpec((tm, d), lambda i: (i, 0)), _layer(w1, j), _whole(b1)],
        out_specs=pl.BlockSpec((tm, d), lambda i: (i, 0)),
        out_shape=jax.ShapeDtypeStruct((n, d), F32),
        scratch_shapes=[pltpu.VMEM((d, 2 * d), BF16)],
        compiler_params=_cparams(("arbitrary",)),
        name="conv_glu",
    )(x, w1, b1)


def _conv_tail(c, x, dwb, lng, lnb, w2b, b2, g, b, alpha):
    c = _ln(c + dwb, lng, lnb)
    s = (c * jax.nn.sigmoid(c)).astype(BF16)
    y = _dot(s, w2b) + b2
    return _ln(alpha * x + y, g, b)


def _conv_chunk(ext_ref, dw_ref, c_ref, base, rc, off, lw):
    kw, d_model = dw_ref.shape
    for lo in range(0, d_model, lw):
        acc = None
        for b in range(SUBLANES):
            vb = None
            for a in range((kw + off) // SUBLANES + 1):
                k = SUBLANES * a + b - off
                if 0 <= k < kw:
                    term = (ext_ref[pl.ds(base + SUBLANES * a, rc + SUBLANES), lo:lo + lw]
                            * dw_ref[k:k + 1, lo:lo + lw])
                    vb = term if vb is None else vb + term
            if vb is not None:
                sh = vb[b:b + rc]
                acc = sh if acc is None else acc + sh
        c_ref[pl.ds(base, rc), lo:lo + lw] = acc


def _conv_prompt_kernel(u_ref, x_ref, dw_ref, dwb_ref, lng_ref, lnb_ref, w2_ref, b2_ref, g_ref, b_ref,
                        o_ref, ext_ref, c_ref, w2b_ref, *, tt, ctx, rc, alpha, j, li):
    i = pl.program_id(0)
    t = pl.program_id(1)
    d_model = x_ref.shape[-1]
    kw = dw_ref.shape[0]
    off = ctx - (kw - 1)
    lw = min(256, d_model)

    @pl.when((i == 0) & (t == 0))
    def _():
        w2b_ref[...] = w2_ref[...].astype(BF16)
        ext_ref[ctx + tt:ctx + tt + SUBLANES, :] = jnp.zeros((SUBLANES, d_model), F32)

    @pl.when(t == 0)
    def _():
        ext_ref[0:ctx, :] = jnp.zeros((ctx, d_model), F32)

    @pl.when(t > 0)
    def _():
        ext_ref[0:ctx, :] = ext_ref[tt:tt + ctx, :]

    ext_ref[ctx:ctx + tt, :] = u_ref[...]

    n_full = tt // rc

    def body(ci, carry):
        _conv_chunk(ext_ref, dw_ref, c_ref, pl.multiple_of(ci * rc, SUBLANES), rc, off, lw)
        return carry

    lax.fori_loop(0, n_full, body, 0)
    if tt % rc:
        _conv_chunk(ext_ref, dw_ref, c_ref, n_full * rc, tt % rc, off, lw)

    o_ref[...] = _conv_tail(c_ref[...], x_ref[...], _row(dwb_ref, j), _row(lng_ref, j), _row(lnb_ref, j),
                            w2b_ref[...], _row(b2_ref, j), _row(g_ref, li), _row(b_ref, li), alpha)


def _conv_prompt(u, x, n_out, bsz, tp, dw, dwb, lng, lnb, w2, b2, g, b, alpha, j, li):
    d = x.shape[-1]
    tt = _pick(tp, (688, 512, 256, 128, 64, 32))
    nt = tp // tt
    ctx = 32
    rc = min(64, tt)
    assert dw.shape[1] - 1 <= ctx and tt % SUBLANES == 0
    kern = functools.partial(_conv_prompt_kernel, tt=tt, ctx=ctx, rc=rc, alpha=alpha, j=j, li=li)
    tile = pl.BlockSpec((tt, d), lambda i, t: (i * nt + t, 0))
    return pl.pallas_call(
        kern,
        grid=(bsz, nt),
        in_specs=[tile, tile, _layer(dw, j), _whole(dwb), _whole(lng), _whole(lnb), _layer(w2, j),
                  _whole(b2), _whole(g), _whole(b)],
        out_specs=tile,
        out_shape=jax.ShapeDtypeStruct((n_out, d), F32),
        scratch_shapes=[pltpu.VMEM((tt + ctx + SUBLANES, d), F32), pltpu.VMEM((tt, d), F32),
                        pltpu.VMEM((d, d), BF16)],
        compiler_params=_cparams(("arbitrary", "arbitrary")),
        name="conv_prompt",
    )(u, x, dw, dwb, lng, lnb, w2, b2, g, b)


def _conv_sample_kernel(ext_ref, x_ref, dw_ref, dwb_ref, lng_ref, lnb_ref, w2_ref, b2_ref,
                        g_ref, b_ref, flat_ref, o_ref, *, alpha, j, li):
    del flat_ref
    t = pl.program_id(0)
    kw = dw_ref.shape[0]
    acc = ext_ref[t] * dw_ref[0:1, :]
    for k in range(1, kw):
        acc = acc + ext_ref[t + k] * dw_ref[k:k + 1, :]
    o_ref[...] = _conv_tail(acc, x_ref[...], _row(dwb_ref, j), _row(lng_ref, j), _row(lnb_ref, j),
                            w2_ref[...].astype(BF16), _row(b2_ref, j), _row(g_ref, li), _row(b_ref, li),
                            alpha)


def _conv_sample(ext, x, flat, n_p, dw, dwb, lng, lnb, w2, b2, g, b, alpha, j, li):
    tot, ns, d = ext.shape
    ts = tot - (dw.shape[1] - 1)
    assert n_p % ns == 0
    kern = functools.partial(_conv_sample_kernel, alpha=alpha, j=j, li=li)
    rows = pl.BlockSpec((ns, d), lambda t: (n_p // ns + t, 0))
    return pl.pallas_call(
        kern,
        grid=(ts,),
        in_specs=[_whole(ext), rows, _layer(dw, j), _whole(dwb), _whole(lng), _whole(lnb), _layer(w2, j),
                  _whole(b2), _whole(g), _whole(b), pl.BlockSpec(memory_space=pl.ANY)],
        out_specs=rows,
        out_shape=jax.ShapeDtypeStruct(flat.shape, F32),
        input_output_aliases={10: 0},
        compiler_params=_cparams(("arbitrary",)),
        name="conv_sample",
    )(ext, x, dw, dwb, lng, lnb, w2, b2, g, b, flat)


def _route_kernel(x_ref, r_ref, route_ref, rt_ref, cnt_ref, tri_ref, carry_ref, *, n_exp):
    i = pl.program_id(0)
    tm = x_ref.shape[0]

    @pl.when(i == 0)
    def _():
        carry_ref[...] = jnp.zeros(carry_ref.shape, F32)
        rr = lax.broadcasted_iota(I32, (tm, tm), 0)
        cc = lax.broadcasted_iota(I32, (tm, tm), 1)
        tri_ref[...] = jnp.where(rr > cc, 1.0, 0.0).astype(BF16)

    x = x_ref[...]
    r = r_ref[...]
    xh = x.astype(BF16)
    xl = (x - xh.astype(F32)).astype(BF16)
    rh = r.astype(BF16)
    rl = (r - rh.astype(F32)).astype(BF16)
    logits = _dot(xh, rh) + (_dot(xl, rh) + _dot(xh, rl))
    lane = lax.broadcasted_iota(I32, (tm, LANES), 1).astype(F32)
    l1 = jnp.where(lane < n_exp, logits, NEG)
    m1 = jnp.max(l1, axis=-1, keepdims=True)
    i1 = jnp.min(jnp.where(l1 == m1, lane, float(LANES)), axis=-1, keepdims=True)
    l2 = jnp.where(lane == i1, NEG, l1)
    m2 = jnp.max(l2, axis=-1, keepdims=True)
    i2 = jnp.min(jnp.where(l2 == m2, lane, float(LANES)), axis=-1, keepdims=True)
    ex = jnp.exp(m2 - m1)
    p1 = 1.0 / (1.0 + ex)
    p2 = ex * p1
    oh1 = lane == i1
    oh2 = lane == i2
    m = jnp.where(oh1 | oh2, 1.0, 0.0)
    before = _dot(tri_ref[...], m.astype(BF16)) + carry_ref[...]
    rank1 = jnp.sum(jnp.where(oh1, before, 0.0), axis=-1, keepdims=True)
    rank2 = jnp.sum(jnp.where(oh2, before, 0.0), axis=-1, keepdims=True)
    carry_ref[...] += jnp.sum(m, axis=0, keepdims=True)
    route = jnp.where(lane == 0, i1, 0.0)
    for k, v in enumerate((i2, p1, p2, rank1, rank2), start=1):
        route = jnp.where(lane == k, v, route)
    route_ref[...] = route
    rt_ref[...] = jnp.transpose(route)[0:SUBLANES, :]
    cnt_ref[...] = carry_ref[...]


def _route(x, router, j):
    n, d = x.shape
    n_exp = router.shape[2]
    tm = _pick(n, (896, 512, 256, 128, 64, 32, 16, 8))
    rp = jnp.pad(router, ((0, 0), (0, 0), (0, LANES - n_exp)))
    kern = functools.partial(_route_kernel, n_exp=n_exp)
    return pl.pallas_call(
        kern,
        grid=(n // tm,),
        in_specs=[pl.BlockSpec((tm, d), lambda i: (i, 0)), _layer(rp, j)],
        out_specs=[pl.BlockSpec((tm, LANES), lambda i: (i, 0)),
                   pl.BlockSpec((SUBLANES, tm), lambda i: (0, i)),
                   pl.BlockSpec((1, LANES), lambda i: (0, 0))],
        out_shape=[jax.ShapeDtypeStruct((n, LANES), F32), jax.ShapeDtypeStruct((SUBLANES, n), F32),
                   jax.ShapeDtypeStruct((1, LANES), F32)],
        scratch_shapes=[pltpu.VMEM((tm, tm), BF16), pltpu.VMEM((1, LANES), F32)],
        compiler_params=_cparams(("arbitrary",)),
        name="route",
    )(x, rp)


def _invert_kernel(p1_ref, p2_ref, st_ref, cn_ref, pd_ref, src_ref, dst_ref, *, n, tms):
    n_exp = st_ref.shape[0]
    for e in range(n_exp):
        def pad_body(s, carry):
            src_ref[s] = 0
            dst_ref[s] = 2 * n + lax.rem(s, tms)
            return carry

        lax.fori_loop(st_ref[e] + cn_ref[e], st_ref[e] + pd_ref[e], pad_body, 0)

    unroll = SUBLANES if n % SUBLANES == 0 else 1

    def body(c, carry):
        toks = [c * unroll + k for k in range(unroll)]
        slots = [(p1_ref[i], p2_ref[i]) for i in toks]
        for i, (a, b) in zip(toks, slots):
            src_ref[a] = i
            dst_ref[a] = i
            src_ref[b] = i
            dst_ref[b] = n + i
        return carry

    lax.fori_loop(0, n // unroll, body, 0)


def _invert(pos1, pos2, starts, counts, padded, p, tms):
    n = pos1.shape[0]
    smem = pl.BlockSpec(memory_space=pltpu.SMEM)
    kern = functools.partial(_invert_kernel, n=n, tms=tms)
    return pl.pallas_call(
        kern,
        in_specs=[smem] * 5,
        out_specs=[smem, smem],
        out_shape=[jax.ShapeDtypeStruct((p,), I32), jax.ShapeDtypeStruct((p,), I32)],
        name="invert",
    )(pos1, pos2, starts, counts, padded)


def _row_copy_wait(src, dst, sem):
    pltpu.make_async_copy(src, dst, sem).wait()


def _gather_rows_kernel(rb_ref, act_ref, src_ref, x_hbm, o_ref, sem, *, tms):
    t = pl.program_id(0)
    unroll = SUBLANES

    @pl.when(act_ref[t] == 1)
    def _():
        base = t * tms

        def body(i, carry):
            for k in range(unroll):
                r = i * unroll + k
                pltpu.make_async_copy(x_hbm.at[pl.ds(src_ref[base + r], 1)],
                                      o_ref.at[pl.ds(r, 1)], sem).start()
            return carry

        lax.fori_loop(0, tms // unroll, body, 0)
        _row_copy_wait(x_hbm.at[pl.ds(0, tms)], o_ref, sem)


def _gather_rows(rb, act, src, x, p, tms):
    d = x.shape[1]
    grid_spec = pltpu.PrefetchScalarGridSpec(
        num_scalar_prefetch=3,
        grid=(p // tms,),
        in_specs=[pl.BlockSpec(memory_space=pl.ANY)],
        out_specs=pl.BlockSpec((tms, d), lambda t, rb, act, src: (rb[t], 0)),
        scratch_shapes=[pltpu.SemaphoreType.DMA(())],
    )
    return pl.pallas_call(
        functools.partial(_gather_rows_kernel, tms=tms),
        grid_spec=grid_spec,
        out_shape=jax.ShapeDtypeStruct((p, d), F32),
        compiler_params=_cparams(("arbitrary",)),
        name="gather_rows",
    )(rb, act, src, x)


def _scatter_rows_kernel(rb_ref, act_ref, dst_ref, ys_ref, y2_hbm, sem, *, tms):
    t = pl.program_id(0)
    unroll = SUBLANES

    @pl.when(act_ref[t] == 1)
    def _():
        base = t * tms

        def body(i, carry):
            for k in range(unroll):
                r = i * unroll + k
                pltpu.make_async_copy(ys_ref.at[pl.ds(r, 1)],
                                      y2_hbm.at[pl.ds(dst_ref[base + r], 1)], sem).start()
            return carry

        lax.fori_loop(0, tms // unroll, body, 0)
        _row_copy_wait(ys_ref, y2_hbm.at[pl.ds(0, tms)], sem)


def _scatter_rows(rb, act, dst, ys, n_rows, tms):
    p, d = ys.shape
    grid_spec = pltpu.PrefetchScalarGridSpec(
        num_scalar_prefetch=3,
        grid=(p // tms,),
        in_specs=[pl.BlockSpec((tms, d), lambda t, rb, act, dst: (rb[t], 0))],
        out_specs=pl.BlockSpec(memory_space=pl.ANY),
        scratch_shapes=[pltpu.SemaphoreType.DMA(())],
    )
    return pl.pallas_call(
        functools.partial(_scatter_rows_kernel, tms=tms),
        grid_spec=grid_spec,
        out_shape=jax.ShapeDtypeStruct((n_rows, d), F32),
        compiler_params=_cparams(("arbitrary",)),
        name="scatter_rows",
    )(rb, act, dst, ys)


def _combine_ln_kernel(x_ref, g1_ref, g2_ref, route_ref, g_ref, b_ref, o_ref, *, alpha, li):
    p1 = route_ref[:, 2:3]
    p2 = route_ref[:, 3:4]
    y = p1 * g1_ref[...] + p2 * g2_ref[...]
    o_ref[...] = _ln(alpha * x_ref[...] + y, _row(g_ref, li), _row(b_ref, li))


def _combine_ln(x, y2, route, g, b, alpha, li):
    n, d = x.shape
    tm = _pick(n, (896, 512, 256, 128, 64, 32, 16, 8))
    nb = n // tm
    kern = functools.partial(_combine_ln_kernel, alpha=alpha, li=li)
    tile = pl.BlockSpec((tm, d), lambda i: (i, 0))
    return pl.pallas_call(
        kern,
        grid=(nb,),
        in_specs=[tile, tile, pl.BlockSpec((tm, d), lambda i: (nb + i, 0)),
                  pl.BlockSpec((tm, LANES), lambda i: (i, 0)), _whole(g), _whole(b)],
        out_specs=tile,
        out_shape=jax.ShapeDtypeStruct((n, d), F32),
        compiler_params=_cparams(("arbitrary",)),
        name="combine_ln",
    )(x, y2, y2, route, g, b)


def _moe(x, router, wg, wu, wd, g, b, alpha, j, li):
    n, d = x.shape
    n_exp = router.shape[2]
    tms = 1024 if n >= 4096 else 32
    route, rt, cnt = _route(x, router, j)
    e1 = rt[0].astype(I32)
    e2 = rt[1].astype(I32)
    counts = cnt[0, :n_exp].astype(I32)
    padded = ((counts + tms - 1) // tms) * tms
    ends = jnp.cumsum(padded)
    starts = ends - padded
    pos1 = starts[e1] + rt[4].astype(I32)
    pos2 = starts[e2] + rt[5].astype(I32)
    n_tiles = (2 * n + n_exp * (tms - 1)) // tms
    p = n_tiles * tms
    n_act = ends[-1] // tms
    tiles = jnp.arange(n_tiles, dtype=I32)
    rb = jnp.minimum(tiles, n_act - 1)
    te = jnp.sum((rb[:, None] * tms >= ends[None, :]).astype(I32), axis=1)
    act = (tiles < n_act).astype(I32)
    src, dst = _invert(pos1, pos2, starts, counts, padded, p, tms)
    xs = _gather_rows(rb, act, src, x, p, tms)
    ys = _moe_ffn(te, rb, act, xs, wg, wu, wd, tms, j)
    y2 = _scatter_rows(rb, act, dst, ys, 2 * n + tms, tms)
    return _combine_ln(x, y2, route, g, b, alpha, li)


def _seq_rows(x, bsz, tp, lo, hi):
    return jnp.stack([x[i * tp + lo:i * tp + hi] for i in range(bsz)])


def kernel(x_prompt, x_sample, state_pool, state_conv, meta, pool_w, pool_scale, conv_w1, conv_b1, conv_dw, conv_dw_b, conv_ln_g, conv_ln_b, conv_w2, conv_b2, ln_mix_g, ln_mix_b, ln_ffn_g, ln_ffn_b, ffn_w_gate, ffn_w_up, ffn_w_down, moe_router, moe_w_gate, moe_w_up, moe_w_down):
    bsz, seq, d = x_prompt.shape
    ns, ts, _ = x_sample.shape
    depth = ln_mix_g.shape[0]
    n_meta = meta.shape[0]
    pctx = state_pool.shape[2]
    cctx = state_conv.shape[2]
    assert pctx == max(POOL_WINDOWS) - 1 and cctx == conv_dw.shape[1] - 1
    alpha = float((2 * depth) ** 0.25)
    tp = seq + n_meta
    n_p = bsz * tp
    n = n_p + ns * ts

    meta_b = jnp.broadcast_to(meta[None], (bsz, n_meta, d))
    x = jnp.concatenate([meta_b, x_prompt], axis=1).reshape(n_p, d)
    xs_t = x_sample.transpose(1, 0, 2)

    pool_p, conv_p, pool_s, conv_s = [], [], [], []
    for i in range(depth):
        j = i // 2
        if i > 0:
            xs_t = x[n_p:].reshape(ts, ns, d)
        if i % 2 == 0:
            pool_p.append(_seq_rows(x, bsz, tp, tp - pctx, tp))
            pool_s.append(jnp.concatenate([state_pool[j][:, ts:], xs_t.transpose(1, 0, 2)], axis=1))
            args = (pool_w, pool_scale, ln_mix_g, ln_mix_b, alpha, j, i)
            ext = jnp.concatenate([state_pool[j].transpose(1, 0, 2), xs_t], axis=0)
            y = _pool_prompt(x, n, bsz, tp, *args)
            x = _pool_sample(ext, y, n_p, pctx, *args)
            x = _ffn_dense(x, ffn_w_gate, ffn_w_up, ffn_w_down, ln_ffn_g, ln_ffn_b, alpha, j, i)
        else:
            u = _glu(x, conv_w1, conv_b1, j)
            us_t = u[n_p:].reshape(ts, ns, d)
            conv_p.append(_seq_rows(u, bsz, tp, tp - cctx, tp))
            conv_s.append(jnp.concatenate([state_conv[j][:, ts:], us_t.transpose(1, 0, 2)], axis=1))
            args = (conv_dw, conv_dw_b, conv_ln_g, conv_ln_b, conv_w2, conv_b2, ln_mix_g, ln_mix_b,
                    alpha, j, i)
            ext = jnp.concatenate([state_conv[j].transpose(1, 0, 2), us_t], axis=0)
            y = _conv_prompt(u, x, n, bsz, tp, *args)
            x = _conv_sample(ext, x, y, n_p, *args)
            x = _moe(x, moe_router, moe_w_gate, moe_w_up, moe_w_down, ln_ffn_g, ln_ffn_b, alpha, j, i)

    y_prompt = _seq_rows(x, bsz, tp, n_meta, tp)
    y_sample = x[n_p:].reshape(ts, ns, d).transpose(1, 0, 2)
    return (y_prompt, y_sample, jnp.stack(pool_p), jnp.stack(conv_p), jnp.stack(pool_s), jnp.stack(conv_s))
```
